```python
import math
import jax, jax.numpy as jnp
from jax import lax
import numpy as np

D_MODEL = 1024
BATCH = 8
SEQ = 2048
DEPTH = 2
DEC_BATCH = 128
DEC_SEQ = 4
PAST_LEN = 16384
PAGE_SIZE = 128

N_MIXERS = 2
N_RET_LAYERS = (DEPTH + 1) // 2
N_SWA_LAYERS = DEPTH // 2
RET_HEADS = 4
RET_DK = 256
RET_DV = 512
RET_CHUNK = 128
SWA_Q_HEADS = 16
SWA_KV_HEADS = 4
SWA_GROUP = SWA_Q_HEADS // SWA_KV_HEADS
SWA_HEAD_DIM = 64
WINDOW = 128
SWA_BLOCK = 128
ROPE_THETA = 10000.0
D_FF = -(-8 * D_MODEL // (3 * 256)) * 256
EPS = 1e-6

kernel_name = "retnet_swa_sink_hybrid_step"

F32 = jnp.float32


def rms_norm(x, gain=None):
    xf = x.astype(F32)
    y = xf * lax.rsqrt(jnp.mean(xf * xf, axis=-1, keepdims=True) + EPS)
    if gain is not None:
        y = y * gain.astype(F32)
    return y.astype(x.dtype)


def rope(x, pos, inv_freq):
    ang = pos[:, None] * inv_freq[None, :]
    cos = jnp.cos(ang)[:, None, :]
    sin = jnp.sin(ang)[:, None, :]
    xf = x.astype(F32)
    half = xf.shape[-1] // 2
    x1, x2 = xf[..., :half], xf[..., half:]
    return jnp.concatenate([x1 * cos - x2 * sin, x2 * cos + x1 * sin], axis=-1).astype(x.dtype)


def ret_inv_freq():
    return 1.0 / (ROPE_THETA ** jnp.linspace(0.0, 1.0, RET_DK // 2, dtype=F32))


def swa_inv_freq():
    return 1.0 / (ROPE_THETA ** (jnp.arange(0, SWA_HEAD_DIM, 2, dtype=F32) / SWA_HEAD_DIM))


def ret_log_decay():
    return jnp.log(1.0 - 2.0 ** (-5.0 - jnp.arange(RET_HEADS, dtype=F32)))


def retention_chunk(S, q, k, v, lg):
    C = q.shape[1]
    idx = jnp.arange(C, dtype=F32)
    diff = idx[:, None] - idx[None, :]
    decay = jnp.where(diff >= 0, jnp.exp(jnp.maximum(diff, 0.0)[None] * lg[:, None, None]), 0.0)
    scores = jnp.einsum('bihd,bjhd->bhij', q, k) * decay
    o = jnp.einsum('bhij,bjhe->bihe', scores, v)
    q_dec = q * jnp.exp((idx + 1.0)[:, None] * lg[None, :])[None, :, :, None]
    o = o + jnp.einsum('bihd,bhde->bihe', q_dec, S)
    k_dec = k * jnp.exp((C - 1.0 - idx)[:, None] * lg[None, :])[None, :, :, None]
    S_new = jnp.exp(C * lg)[None, :, None, None] * S + jnp.einsum('bjhd,bjhe->bhde', k_dec, v)
    return S_new, o


def retention_mixer(h, S0, w_in, w_out, pos0):
    B, T, _ = h.shape
    hk, hv = RET_HEADS * RET_DK, RET_HEADS * RET_DV
    proj = h @ w_in
    q = proj[..., :hk].reshape(B, T, RET_HEADS, RET_DK)
    k = proj[..., hk:2 * hk].reshape(B, T, RET_HEADS, RET_DK)
    v = proj[..., 2 * hk:2 * hk + hv].reshape(B, T, RET_HEADS, RET_DV)
    g = proj[..., 2 * hk + hv:]
    pos = pos0 + jnp.arange(T, dtype=F32)
    inv = ret_inv_freq()
    q = rope(q, pos, inv).astype(F32)
    k = rope(k, pos, inv).astype(F32) * (RET_DK ** -0.5)
    v = v.astype(F32)
    lg = ret_log_decay()
    C = RET_CHUNK if T % RET_CHUNK == 0 else T
    nc = T // C
    to_chunks = lambda a: jnp.moveaxis(a.reshape(B, nc, C, *a.shape[2:]), 1, 0)

    def step(S, xs):
        qc, kc, vc = xs
        return retention_chunk(S, qc, kc, vc, lg)

    S_fin, o = lax.scan(step, S0.astype(F32), (to_chunks(q), to_chunks(k), to_chunks(v)))
    o = jnp.moveaxis(o, 0, 1).reshape(B, T, RET_HEADS, RET_DV)
    o = rms_norm(o).reshape(B, T, hv)
    y = (jax.nn.silu(g.astype(F32)) * o).astype(h.dtype) @ w_out
    return y, S_fin


def swa_qkv(h, w_in, q_gain, k_gain, pos):
    B, T, _ = h.shape
    nq, nk = SWA_Q_HEADS * SWA_HEAD_DIM, SWA_KV_HEADS * SWA_HEAD_DIM
    proj = h @ w_in
    q = proj[..., :nq].reshape(B, T, SWA_Q_HEADS, SWA_HEAD_DIM)
    k = proj[..., nq:nq + nk].reshape(B, T, SWA_KV_HEADS, SWA_HEAD_DIM)
    v = proj[..., nq + nk:].reshape(B, T, SWA_KV_HEADS, SWA_HEAD_DIM)
    inv = swa_inv_freq()
    q = rope(rms_norm(q, q_gain), pos, inv)
    k = rope(rms_norm(k, k_gain), pos, inv)
    return q, k, v


def sink_attention(q, k, v, mask, sinks):
    s = jnp.einsum('bnqhgd,bnkhd->bnhgqk', q, k).astype(F32) * (SWA_HEAD_DIM ** -0.5)
    s = jnp.where(mask[None, :, None, None], s, -jnp.inf)
    sink = jnp.broadcast_to(sinks.astype(F32).reshape(SWA_KV_HEADS, SWA_GROUP)[None, None, :, :, None, None],
                            s.shape[:-1] + (1,))
    p = jax.nn.softmax(jnp.concatenate([s, sink], axis=-1), axis=-1)[..., :-1]
    return jnp.einsum('bnhgqk,bnkhd->bnqhgd', p.astype(v.dtype), v)


def swa_prompt(h, w_in, w_out, q_gain, k_gain, sinks):
    B, T, _ = h.shape
    q, k, v = swa_qkv(h, w_in, q_gain, k_gain, jnp.arange(T, dtype=F32))
    nb = T // SWA_BLOCK
    qb = q.reshape(B, nb, SWA_BLOCK, SWA_KV_HEADS, SWA_GROUP, SWA_HEAD_DIM)
    kb = k.reshape(B, nb, SWA_BLOCK, SWA_KV_HEADS, SWA_HEAD_DIM)
    vb = v.reshape(B, nb, SWA_BLOCK, SWA_KV_HEADS, SWA_HEAD_DIM)
    band = lambda a: jnp.concatenate([jnp.concatenate([jnp.zeros_like(a[:, :1]), a[:, :-1]], axis=1), a], axis=2)
    kband, vband = band(kb), band(vb)
    i = jnp.arange(SWA_BLOCK)[:, None]
    j = jnp.arange(2 * SWA_BLOCK)[None, :]
    d = SWA_BLOCK + i - j
    n = jnp.arange(nb)[:, None, None]
    mask = (d >= 0) & (d <= WINDOW) & ((n > 0) | (j >= SWA_BLOCK))
    o = sink_attention(qb, kband, vband, mask, sinks).reshape(B, T, SWA_Q_HEADS * SWA_HEAD_DIM)
    L = min(WINDOW, T)
    return o @ w_out, k[:, T - L:], v[:, T - L:]


def swa_sample(h, cache_k, cache_v, w_in, w_out, q_gain, k_gain, sinks):
    B, T, _ = h.shape
    q, k, v = swa_qkv(h, w_in, q_gain, k_gain, PAST_LEN + jnp.arange(T, dtype=F32))
    L = cache_k.shape[1]
    kall = jnp.concatenate([cache_k.astype(k.dtype), k], axis=1)
    vall = jnp.concatenate([cache_v.astype(v.dtype), v], axis=1)
    i = jnp.arange(T)[:, None]
    j = jnp.arange(L + T)[None, :]
    d = i - j + L
    mask = ((d >= 0) & (d <= WINDOW))[None]
    qb = q.reshape(B, 1, T, SWA_KV_HEADS, SWA_GROUP, SWA_HEAD_DIM)
    o = sink_attention(qb, kall[:, None], vall[:, None], mask, sinks).reshape(B, T, SWA_Q_HEADS * SWA_HEAD_DIM)
    return o @ w_out, kall[:, T:], vall[:, T:]


def swiglu(h, w_in, w_out):
    gu = h @ w_in
    g, u = gu[..., :D_FF], gu[..., D_FF:]
    return (jax.nn.silu(g.astype(F32)) * u.astype(F32)).astype(h.dtype) @ w_out


def setup_inputs(seed: int = 0) -> dict:
    key = jax.random.key(seed)
    ks = jax.random.split(key, 16)
    nrm = lambda k, shape, scale: jax.random.normal(k, shape, F32) * scale
    ret_in_w = 2 * RET_HEADS * RET_DK + 2 * RET_HEADS * RET_DV
    swa_in_w = (SWA_Q_HEADS + 2 * SWA_KV_HEADS) * SWA_HEAD_DIM
    buf = min(WINDOW, PAST_LEN)
    return {
        "x_prompt": nrm(ks[0], (BATCH, SEQ, D_MODEL), 1.0),
        "x_sample": nrm(ks[1], (DEC_BATCH, DEC_SEQ, D_MODEL), 1.0),
        "state_ret": nrm(ks[2], (N_RET_LAYERS, DEC_BATCH, RET_HEADS, RET_DK, RET_DV), 0.3),
        "cache_swa_k": nrm(ks[3], (N_SWA_LAYERS, DEC_BATCH, buf, SWA_KV_HEADS, SWA_HEAD_DIM), 1.0),
        "cache_swa_v": nrm(ks[4], (N_SWA_LAYERS, DEC_BATCH, buf, SWA_KV_HEADS, SWA_HEAD_DIM), 1.0),
        "norm_mix": 1.0 + nrm(ks[5], (DEPTH, D_MODEL), 0.02),
        "norm_ffn": 1.0 + nrm(ks[6], (DEPTH, D_MODEL), 0.02),
        "w_ret_in": nrm(ks[7], (N_RET_LAYERS, D_MODEL, ret_in_w), D_MODEL ** -0.5),
        "w_ret_out": nrm(ks[8], (N_RET_LAYERS, RET_HEADS * RET_DV, D_MODEL), (RET_HEADS * RET_DV) ** -0.5),
        "w_swa_in": nrm(ks[9], (N_SWA_LAYERS, D_MODEL, swa_in_w), D_MODEL ** -0.5),
        "w_swa_out": nrm(ks[10], (N_SWA_LAYERS, SWA_Q_HEADS * SWA_HEAD_DIM, D_MODEL), (SWA_Q_HEADS * SWA_HEAD_DIM) ** -0.5),
        "swa_q_norm": 1.0 + nrm(ks[11], (N_SWA_LAYERS, SWA_HEAD_DIM), 0.02),
        "swa_k_norm": 1.0 + nrm(ks[12], (N_SWA_LAYERS, SWA_HEAD_DIM), 0.02),
        "swa_sinks": nrm(ks[13], (N_SWA_LAYERS, SWA_Q_HEADS), 1.0),
        "w_ffn_in": nrm(ks[14], (DEPTH, D_MODEL, 2 * D_FF), D_MODEL ** -0.5),
        "w_ffn_out": nrm(ks[15], (DEPTH, D_FF, D_MODEL), D_FF ** -0.5),
    }


def reference(x_prompt, x_sample, state_ret, cache_swa_k, cache_swa_v, norm_mix, norm_ffn,
              w_ret_in, w_ret_out, w_swa_in, w_swa_out, swa_q_norm, swa_k_norm, swa_sinks,
              w_ffn_in, w_ffn_out):
    xp, xs = x_prompt, x_sample
    ret_p, ret_s, kp, vp, ksn, vsn = [], [], [], [], [], []
    for layer in range(DEPTH):
        hp = rms_norm(xp, norm_mix[layer])
        hs = rms_norm(xs, norm_mix[layer])
        if layer % N_MIXERS == 0:
            r = layer // N_MIXERS
            S0 = jnp.zeros((xp.shape[0], RET_HEADS, RET_DK, RET_DV), F32)
            yp, Sp = retention_mixer(hp, S0, w_ret_in[r], w_ret_out[r], 0.0)
            ys, Ss = retention_mixer(hs, state_ret[r], w_ret_in[r], w_ret_out[r], float(PAST_LEN))
            ret_p.append(Sp)
            ret_s.append(Ss)
        else:
            a = layer // N_MIXERS
            yp, k_p, v_p = swa_prompt(hp, w_swa_in[a], w_swa_out[a], swa_q_norm[a], swa_k_norm[a], swa_sinks[a])
            ys, k_s, v_s = swa_sample(hs, cache_swa_k[a], cache_swa_v[a], w_swa_in[a], w_swa_out[a],
                                      swa_q_norm[a], swa_k_norm[a], swa_sinks[a])
            kp.append(k_p)
            vp.append(v_p)
            ksn.append(k_s)
            vsn.append(v_s)
        xp = xp + yp
        xs = xs + ys
        xp = xp + swiglu(rms_norm(xp, norm_ffn[layer]), w_ffn_in[layer], w_ffn_out[layer])
        xs = xs + swiglu(rms_norm(xs, norm_ffn[layer]), w_ffn_in[layer], w_ffn_out[layer])
    new_state_ret_p = jnp.stack(ret_p, 0)
    new_state_ret_s = jnp.stack(ret_s, 0)
    new_cache_swa_k_p = jnp.stack(kp, 0)
    new_cache_swa_v_p = jnp.stack(vp, 0)
    new_cache_swa_k_s = jnp.stack(ksn, 0)
    new_cache_swa_v_s = jnp.stack(vsn, 0)
    return (xp, xs, new_state_ret_p, new_state_ret_s, new_cache_swa_k_p, new_cache_swa_v_p, new_cache_swa_k_s, new_cache_swa_v_s)
```

```python
import functools

import numpy as np
import jax
import jax.numpy as jnp
from jax import lax
from jax.experimental import pallas as pl
from jax.experimental.pallas import tpu as pltpu

F32 = jnp.float32
BF16 = jnp.bfloat16

RET_HEADS = 4
RET_DK = 256
RET_DV = 512
RET_CHUNK = 128
SWA_Q_HEADS = 16
SWA_KV_HEADS = 4
SWA_GROUP = SWA_Q_HEADS // SWA_KV_HEADS
SWA_HEAD_DIM = 64
WINDOW = 128
SWA_BLOCK = 128
ROPE_THETA = 10000.0
EPS = 1e-6
PAST_LEN = 16384

V7X_LANES = 128
V7X_VMEM_BYTES = 64 * 1024 * 1024
VMEM_REQUEST_CAP = V7X_VMEM_BYTES - 8 * 1024 * 1024

MASK_VALUE = -1e30

_NT = (((1,), (1,)), ((), ()))
_TN = (((0,), (0,)), ((), ()))


def _params(semantics, vmem_bytes):
    return pltpu.CompilerParams(
        dimension_semantics=semantics,
        vmem_limit_bytes=int(min(max(vmem_bytes, 16 * 1024 * 1024), VMEM_REQUEST_CAP)),
    )


def _resident(shape):
    zeros = (0,) * len(shape)
    return pl.BlockSpec(shape, lambda *_: zeros, pipeline_mode=pl.Buffered(1))


def _sigmoid(x):
    return 1.0 / (1.0 + jnp.exp(-x))


def _rms(x, gain=None):
    y = x * lax.rsqrt(jnp.mean(x * x, axis=-1, keepdims=True) + EPS)
    return y if gain is None else y * gain


def _norm_proj_kernel(x_ref, g_ref, w_ref, o_ref, h_ref):
    @pl.when(pl.program_id(1) == 0)
    def _():
        h_ref[...] = _rms(x_ref[...], g_ref[...]).astype(BF16)

    o_ref[...] = jnp.dot(h_ref[...], w_ref[...], preferred_element_type=F32).astype(o_ref.dtype)


def _norm_proj(x, gain, w, *, tm, tn):
    m, d = x.shape
    n = w.shape[1]
    vmem = 2 * (tm * d * 4 + d * tn * 2 + tm * tn * 2) + tm * d * 2 + tm * tn * 4 + tm * d * 4
    return pl.pallas_call(
        _norm_proj_kernel,
        grid=(m // tm, n // tn),
        in_specs=[
            pl.BlockSpec((tm, d), lambda i, j: (i, 0)),
            pl.BlockSpec((1, d), lambda i, j: (0, 0)),
            pl.BlockSpec((d, tn), lambda i, j: (0, j)),
        ],
        out_specs=pl.BlockSpec((tm, tn), lambda i, j: (i, j)),
        out_shape=jax.ShapeDtypeStruct((m, n), BF16),
        scratch_shapes=[pltpu.VMEM((tm, d), BF16)],
        compiler_params=_params(("parallel", "arbitrary"), vmem),
        name="norm_proj",
    )(x, gain.reshape(1, d), w)


def _post_kernel(x_ref, a_ref, wo_ref, g_ref, wi_ref, wd_ref, o_ref, *, n_chunks):
    x1 = x_ref[...] + jnp.dot(a_ref[...].astype(BF16), wo_ref[...], preferred_element_type=F32)
    h = _rms(x1, g_ref[...]).astype(BF16)
    d_ff = wd_ref.shape[0]
    tf = d_ff // n_chunks
    acc = x1
    for c in range(n_chunks):
        g = jnp.dot(h, wi_ref[:, c * tf:(c + 1) * tf], preferred_element_type=F32)
        u = jnp.dot(h, wi_ref[:, d_ff + c * tf:d_ff + (c + 1) * tf], preferred_element_type=F32)
        act = (g * _sigmoid(g) * u).astype(BF16)
        acc = acc + jnp.dot(act, wd_ref[c * tf:(c + 1) * tf, :], preferred_element_type=F32)
    o_ref[...] = acc


def _post(x, a, wo, gain, wi, wd, *, tm, n_chunks):
    m, d = x.shape
    ka = a.shape[1]
    d_ff = wd.shape[0]
    tf = d_ff // n_chunks
    vmem = (2 * (2 * tm * d * 4 + tm * ka * a.dtype.itemsize) + (ka * d + d * 2 * d_ff + d_ff * d) * 2
            + tm * d * 10 + tm * tf * 14)
    return pl.pallas_call(
        functools.partial(_post_kernel, n_chunks=n_chunks),
        grid=(m // tm,),
        in_specs=[
            pl.BlockSpec((tm, d), lambda i: (i, 0)),
            pl.BlockSpec((tm, ka), lambda i: (i, 0)),
            _resident((ka, d)),
            _resident((1, d)),
            _resident((d, 2 * d_ff)),
            _resident((d_ff, d)),
        ],
        out_specs=pl.BlockSpec((tm, d), lambda i: (i, 0)),
        out_shape=jax.ShapeDtypeStruct((m, d), F32),
        compiler_params=_params(("parallel",), vmem),
        name="post_ffn",
    )(x, a, wo, gain.reshape(1, d), wi, wd)


def _ret_log_decay():
    h = np.arange(RET_HEADS, dtype=np.float32)
    return np.log(np.float32(1.0) - np.float32(2.0) ** (np.float32(-5.0) - h)).astype(np.float32)


def _ret_decay_tiles(c, lg):
    scale = RET_DK ** -0.5
    row = lax.broadcasted_iota(jnp.int32, (c, c), 0).astype(F32)
    col = lax.broadcasted_iota(jnp.int32, (c, c), 1).astype(F32)
    diff = row - col
    decay = jnp.where(diff >= 0, jnp.exp(jnp.maximum(diff, 0.0) * lg), 0.0) * scale
    rl = lax.broadcasted_iota(jnp.int32, (c, V7X_LANES), 0).astype(F32)
    qdec = jnp.exp((rl + 1.0) * lg)
    kdec = jnp.exp((c - 1.0 - rl) * lg) * scale
    sdec = float(np.exp(np.float32(c) * np.float32(lg)))
    return decay, qdec, kdec, sdec


def _rope_halves(x, cos, sin):
    half = x.shape[-1] // 2
    x1, x2 = x[:, :half], x[:, half:]
    return x1 * cos - x2 * sin, x2 * cos + x1 * sin


def _ret_chunk(q, k, v, g, cos, sin, s, consts):
    decay, qdec, kdec, sdec = consts
    q1, q2 = _rope_halves(q, cos, sin)
    k1, k2 = _rope_halves(k, cos, sin)
    qr = jnp.concatenate([q1, q2], axis=1)
    kr = jnp.concatenate([k1, k2], axis=1)
    qd = jnp.concatenate([q1 * qdec, q2 * qdec], axis=1)
    kd = jnp.concatenate([k1 * kdec, k2 * kdec], axis=1)
    scores = lax.dot_general(qr.astype(BF16), kr.astype(BF16), _NT, preferred_element_type=F32) * decay
    o = jnp.dot(scores.astype(BF16), v, preferred_element_type=F32)
    o = o + jnp.dot(qd.astype(BF16), s.astype(BF16), preferred_element_type=F32)
    s_new = sdec * s + lax.dot_general(kd.astype(BF16), v, _TN, preferred_element_type=F32)
    y = g * _sigmoid(g) * _rms(o)
    return y, s_new


def _ret_prompt_kernel(q_ref, k_ref, v_ref, g_ref, cos_ref, sin_ref, o_ref, so_ref, s_ref, *, lgs):
    tc = pl.program_id(1)

    @pl.when(tc == 0)
    def _():
        s_ref[...] = jnp.zeros(s_ref.shape, F32)

    c = RET_CHUNK
    consts = [_ret_decay_tiles(c, float(lg)) for lg in lgs]

    def body(ci, carry):
        r = pl.multiple_of(ci * c, c)
        cos = cos_ref[pl.ds(r, c), :]
        sin = sin_ref[pl.ds(r, c), :]
        for h in range(RET_HEADS):
            q = q_ref[pl.ds(r, c), h * RET_DK:(h + 1) * RET_DK].astype(F32)
            k = k_ref[pl.ds(r, c), h * RET_DK:(h + 1) * RET_DK].astype(F32)
            v = v_ref[pl.ds(r, c), h * RET_DV:(h + 1) * RET_DV]
            g = g_ref[pl.ds(r, c), h * RET_DV:(h + 1) * RET_DV].astype(F32)
            y, s_new = _ret_chunk(q, k, v, g, cos, sin, s_ref[h], consts[h])
            s_ref[h] = s_new
            o_ref[pl.ds(r, c), h * RET_DV:(h + 1) * RET_DV] = y.astype(o_ref.dtype)
        return carry

    lax.fori_loop(0, q_ref.shape[0] // c, body, 0)

    @pl.when(tc == pl.num_programs(1) - 1)
    def _():
        so_ref[...] = s_ref[...]


def _ret_prompt(proj, cos, sin, *, tt):
    b, t, _ = proj.shape
    hk, hv = RET_HEADS * RET_DK, RET_HEADS * RET_DV
    qspec = pl.BlockSpec((None, tt, hk), lambda i, j: (i, j, 0))
    kspec = pl.BlockSpec((None, tt, hk), lambda i, j: (i, j, 1))
    vspec = pl.BlockSpec((None, tt, hv), lambda i, j: (i, j, 2 * hk // hv))
    gspec = pl.BlockSpec((None, tt, hv), lambda i, j: (i, j, 2 * hk // hv + 1))
    tspec = pl.BlockSpec((tt, RET_DK // 2), lambda i, j: (j, 0))
    vmem = 2 * (2 * tt * hk * 2 + 3 * tt * hv * 2 + 2 * tt * RET_DK * 2) + 3 * RET_HEADS * RET_DK * RET_DV * 4 + (8 << 20)
    return pl.pallas_call(
        functools.partial(_ret_prompt_kernel, lgs=tuple(_ret_log_decay())),
        grid=(b, t // tt),
        in_specs=[qspec, kspec, vspec, gspec, tspec, tspec],
        out_specs=[
            pl.BlockSpec((None, tt, hv), lambda i, j: (i, j, 0)),
            pl.BlockSpec((None, RET_HEADS, RET_DK, RET_DV), lambda i, j: (i, 0, 0, 0)),
        ],
        out_shape=[
            jax.ShapeDtypeStruct((b, t, hv), BF16),
            jax.ShapeDtypeStruct((b, RET_HEADS, RET_DK, RET_DV), F32),
        ],
        scratch_shapes=[pltpu.VMEM((RET_HEADS, RET_DK, RET_DV), F32)],
        compiler_params=_params(("parallel", "arbitrary"), vmem),
        name="ret_prompt",
    )(proj, proj, proj, proj, cos, sin)


def _ret_sample_kernel(p_ref, s_ref, cos_ref, sin_ref, o_ref, so_ref, *, lgs):
    bt, t, _ = p_ref.shape
    hk, hv = RET_HEADS * RET_DK, RET_HEADS * RET_DV
    consts = [_ret_decay_tiles(t, float(lg)) for lg in lgs]
    cos = cos_ref[...]
    sin = sin_ref[...]
    for b in range(bt):
        for h in range(RET_HEADS):
            q = p_ref[b, :, h * RET_DK:(h + 1) * RET_DK].astype(F32)
            k = p_ref[b, :, hk + h * RET_DK:hk + (h + 1) * RET_DK].astype(F32)
            v = p_ref[b, :, 2 * hk + h * RET_DV:2 * hk + (h + 1) * RET_DV]
            g = p_ref[b, :, 2 * hk + hv + h * RET_DV:2 * hk + hv + (h + 1) * RET_DV].astype(F32)
            y, s_new = _ret_chunk(q, k, v, g, cos, sin, s_ref[b, h], consts[h])
            so_ref[b, h] = s_new
            o_ref[b, :, h * RET_DV:(h + 1) * RET_DV] = y


def _ret_sample(proj, states, layer, cos, sin, *, bt):
    b, t, w = proj.shape
    hv = RET_HEADS * RET_DV
    sblock = (bt, RET_HEADS, RET_DK, RET_DV)
    vmem = 4 * bt * RET_HEADS * RET_DK * RET_DV * 4 + (16 << 20)
    return pl.pallas_call(
        functools.partial(_ret_sample_kernel, lgs=tuple(_ret_log_decay())),
        grid=(b // bt,),
        in_specs=[
            pl.BlockSpec((bt, t, w), lambda i: (i, 0, 0)),
            pl.BlockSpec((None,) + sblock, lambda i: (layer, i, 0, 0, 0)),
            pl.BlockSpec((t, RET_DK // 2), lambda i: (0, 0)),
            pl.BlockSpec((t, RET_DK // 2), lambda i: (0, 0)),
        ],
        out_specs=[
            pl.BlockSpec((bt, t, hv), lambda i: (i, 0, 0)),
            pl.BlockSpec(sblock, lambda i: (i, 0, 0, 0)),
        ],
        out_shape=[
            jax.ShapeDtypeStruct((b, t, hv), F32),
            jax.ShapeDtypeStruct(states.shape[1:], F32),
        ],
        compiler_params=_params(("parallel",), vmem),
        name="ret_sample",
    )(proj, states, cos, sin)


def _head_sum_matrix():
    r = lax.broadcasted_iota(jnp.int32, (V7X_LANES, V7X_LANES), 0) // SWA_HEAD_DIM
    c = lax.broadcasted_iota(jnp.int32, (V7X_LANES, V7X_LANES), 1) // SWA_HEAD_DIM
    return jnp.where(r == c, 1.0, 0.0).astype(BF16)


def _qk_norm_rope(x, gain, cos, sin_signed, bd):
    lane = lax.broadcasted_iota(jnp.int32, (x.shape[0], V7X_LANES), 1)
    first_half = (lane % SWA_HEAD_DIM) < (SWA_HEAD_DIM // 2)
    tiles = []
    for t in range(x.shape[1] // V7X_LANES):
        xt = x[:, t * V7X_LANES:(t + 1) * V7X_LANES]
        x2 = xt * xt
        hi = x2.astype(BF16)
        lo = (x2 - hi.astype(F32)).astype(BF16)
        ss = jnp.dot(hi, bd, preferred_element_type=F32) + jnp.dot(lo, bd, preferred_element_type=F32)
        y = xt * lax.rsqrt(ss * (1.0 / SWA_HEAD_DIM) + EPS) * gain
        partner = jnp.where(first_half,
                            pltpu.roll(y, V7X_LANES - SWA_HEAD_DIM // 2, 1),
                            pltpu.roll(y, SWA_HEAD_DIM // 2, 1))
        tiles.append(y * cos + partner * sin_signed)
    return jnp.concatenate(tiles, axis=1)


def _softmax_with_sink(s, sink):
    mx = jnp.maximum(jnp.max(s, axis=-1, keepdims=True), sink)
    e = jnp.exp(s - mx)
    den = jnp.sum(e, axis=-1, keepdims=True) + jnp.exp(sink - mx)
    return e * (1.0 / den)


def _swa_prompt_kernel(sink_ref, p_ref, cos_ref, sin_ref, gq_ref, gk_ref, o_ref, kc_ref, vc_ref,
                       kb_ref, vb_ref, bias_ref):
    blk = SWA_BLOCK
    nq = SWA_Q_HEADS * SWA_HEAD_DIM
    nk = SWA_KV_HEADS * SWA_HEAD_DIM
    nb = p_ref.shape[0] // blk
    bd = _head_sum_matrix()
    gq = gq_ref[...] * (SWA_HEAD_DIM ** -0.5)
    gk = gk_ref[...]

    i = lax.broadcasted_iota(jnp.int32, (blk, 2 * blk), 0)
    j = lax.broadcasted_iota(jnp.int32, (blk, 2 * blk), 1)
    band = jnp.logical_and(j >= blk + i - WINDOW, j <= blk + i)
    bias_ref[0] = jnp.where(jnp.logical_and(band, j >= blk), 0.0, MASK_VALUE)
    bias_ref[1] = jnp.where(band, 0.0, MASK_VALUE)
    kb_ref[:, 0:blk, :] = jnp.zeros((SWA_KV_HEADS, blk, V7X_LANES), BF16)
    vb_ref[:, 0:blk, :] = jnp.zeros((SWA_KV_HEADS, blk, V7X_LANES), BF16)

    lane_head = lax.broadcasted_iota(jnp.int32, (blk, V7X_LANES), 1) // SWA_HEAD_DIM

    def block(n, bias_idx):
        r = pl.multiple_of(n * blk, blk)
        x = p_ref[pl.ds(r, blk), :]
        cos = cos_ref[pl.ds(r, blk), :]
        sin = sin_ref[pl.ds(r, blk), :]
        q = _qk_norm_rope(x[:, :nq].astype(F32), gq, cos, sin, bd)
        k = _qk_norm_rope(x[:, nq:nq + nk].astype(F32), gk, cos, sin, bd)
        v = x[:, nq + nk:].astype(F32)

        @pl.when(n == nb - 1)
        def _():
            kc_ref[...] = k
            vc_ref[...] = v

        for h in range(SWA_KV_HEADS):
            t, hh = divmod(h, V7X_LANES // SWA_HEAD_DIM)
            sel = lane_head == hh
            kb_ref[h, blk:2 * blk, :] = jnp.where(sel, k[:, t * V7X_LANES:(t + 1) * V7X_LANES], 0.0).astype(BF16)
            vb_ref[h, blk:2 * blk, :] = jnp.where(sel, v[:, t * V7X_LANES:(t + 1) * V7X_LANES], 0.0).astype(BF16)

        qb = q.astype(BF16)
        bias = bias_ref[bias_idx]
        acc = [[None, None] for _ in range(SWA_GROUP)]
        for h in range(SWA_KV_HEADS):
            t = h // (V7X_LANES // SWA_HEAD_DIM)
            qs = jnp.concatenate(
                [qb[:, g * nk + t * V7X_LANES:g * nk + (t + 1) * V7X_LANES] for g in range(SWA_GROUP)], axis=0)
            s = lax.dot_general(qs, kb_ref[h], _NT, preferred_element_type=F32)
            for g in range(SWA_GROUP):
                p = _softmax_with_sink(s[g * blk:(g + 1) * blk] + bias, sink_ref[g * SWA_KV_HEADS + h])
                pv = jnp.dot(p.astype(BF16), vb_ref[h], preferred_element_type=F32)
                acc[g][t] = pv if acc[g][t] is None else acc[g][t] + pv
        for g in range(SWA_GROUP):
            for t in range(nk // V7X_LANES):
                o_ref[pl.ds(r, blk), g * nk + t * V7X_LANES:g * nk + (t + 1) * V7X_LANES] = acc[g][t].astype(o_ref.dtype)

        kb_ref[:, 0:blk, :] = kb_ref[:, blk:2 * blk, :]
        vb_ref[:, 0:blk, :] = vb_ref[:, blk:2 * blk, :]

    block(0, 0)

    def body(n, carry):
        block(n, 1)
        return carry

    lax.fori_loop(1, nb, body, 0)


def _swa_prompt(proj, sinks, cos, sin, gq, gk):
    b, t, w = proj.shape
    nq = SWA_Q_HEADS * SWA_HEAD_DIM
    nk = SWA_KV_HEADS * SWA_HEAD_DIM
    blk = SWA_BLOCK
    tab = pl.BlockSpec((t, V7X_LANES), lambda i, s: (0, 0))
    gain = pl.BlockSpec((1, V7X_LANES), lambda i, s: (0, 0))
    cache = pl.BlockSpec((None, blk, nk), lambda i, s: (i, 0, 0))
    vmem = 2 * (t * w * 2 + t * nq * 2 + 2 * t * V7X_LANES * 4) + (16 << 20)
    return pl.pallas_call(
        _swa_prompt_kernel,
        grid_spec=pltpu.PrefetchScalarGridSpec(
            num_scalar_prefetch=1,
            grid=(b,),
            in_specs=[pl.BlockSpec((None, t, w), lambda i, s: (i, 0, 0)), tab, tab, gain, gain],
            out_specs=[pl.BlockSpec((None, t, nq), lambda i, s: (i, 0, 0)), cache, cache],
            scratch_shapes=[
                pltpu.VMEM((SWA_KV_HEADS, 2 * blk, V7X_LANES), BF16),
                pltpu.VMEM((SWA_KV_HEADS, 2 * blk, V7X_LANES), BF16),
                pltpu.VMEM((2, blk, 2 * blk), F32),
            ],
        ),
        out_shape=[
            jax.ShapeDtypeStruct((b, t, nq), BF16),
            jax.ShapeDtypeStruct((b, blk, nk), F32),
            jax.ShapeDtypeStruct((b, blk, nk), F32),
        ],
        compiler_params=_params(("parallel",), vmem),
        name="swa_prompt",
    )(sinks, proj, cos, sin, gq, gk)


def _swa_sample_kernel(sink_ref, p_ref, ck_ref, cv_ref, cos_ref, sin_ref, gq_ref, gk_ref,
                       o_ref, ko_ref, vo_ref, ka_ref, va_ref):
    bt, l, nk = ck_ref.shape
    t = p_ref.shape[0] // bt
    nq = SWA_Q_HEADS * SWA_HEAD_DIM
    rows = SWA_KV_HEADS * SWA_GROUP * t
    pad = ka_ref.shape[0]
    bd = _head_sum_matrix()
    gq = gq_ref[...] * (SWA_HEAD_DIM ** -0.5)
    x = p_ref[...]
    cos = cos_ref[...]
    sin = sin_ref[...]
    q = _qk_norm_rope(x[:, :nq].astype(F32), gq, cos, sin, bd)
    k = _qk_norm_rope(x[:, nq:nq + nk].astype(F32), gk_ref[...], cos, sin, bd)
    v = x[:, nq + nk:].astype(F32)

    ri = lax.broadcasted_iota(jnp.int32, (rows, pad), 0)
    j = lax.broadcasted_iota(jnp.int32, (rows, pad), 1)
    tok = ri % t
    valid = jnp.logical_and(j >= tok + l - WINDOW, j <= tok + l)
    bias = jnp.where(valid, 0.0, MASK_VALUE)
    rcol = lax.broadcasted_iota(jnp.int32, (rows, 1), 0) // t
    sink = jnp.zeros((rows, 1), F32)
    for h in range(SWA_KV_HEADS):
        for g in range(SWA_GROUP):
            sink = jnp.where(rcol == h * SWA_GROUP + g, sink_ref[g * SWA_KV_HEADS + h], sink)
    lane_head = lax.broadcasted_iota(jnp.int32, (SWA_GROUP * t, nk), 1) // SWA_HEAD_DIM

    ka_ref[l + t:pad, :] = jnp.zeros((pad - l - t, nk), F32)
    va_ref[l + t:pad, :] = jnp.zeros((pad - l - t, nk), F32)
    for b in range(bt):
        ka_ref[0:l, :] = ck_ref[b]
        va_ref[0:l, :] = cv_ref[b]
        ka_ref[l:l + t, :] = k[b * t:(b + 1) * t]
        va_ref[l:l + t, :] = v[b * t:(b + 1) * t]
        ko_ref[b] = ka_ref[t:l + t, :]
        vo_ref[b] = va_ref[t:l + t, :]
        qg = jnp.concatenate([q[b * t:(b + 1) * t, g * nk:(g + 1) * nk] for g in range(SWA_GROUP)], axis=0)
        qbd = jnp.concatenate([jnp.where(lane_head == h, qg, 0.0) for h in range(SWA_KV_HEADS)], axis=0)
        s = lax.dot_general(qbd.astype(BF16), ka_ref[...].astype(BF16), _NT, preferred_element_type=F32)
        p = _softmax_with_sink(s + bias, sink)
        o = jnp.dot(p.astype(BF16), va_ref[...].astype(BF16), preferred_element_type=F32)
        gt = SWA_GROUP * t
        og = o[0:gt]
        for h in range(1, SWA_KV_HEADS):
            og = jnp.where(lane_head == h, o[h * gt:(h + 1) * gt], og)
        for g in range(SWA_GROUP):
            o_ref[pl.ds(b * t, t), g * nk:(g + 1) * nk] = og[g * t:(g + 1) * t]


def _swa_sample(proj, cache_k, cache_v, layer, sinks, cos, sin, gq, gk, *, bt):
    _, b, l, nk = cache_k.shape
    m, w = proj.shape
    t = m // b
    nq = SWA_Q_HEADS * SWA_HEAD_DIM
    pad = -(-(l + t) // 8) * 8
    tab = pl.BlockSpec((bt * t, V7X_LANES), lambda i, s: (0, 0))
    gain = pl.BlockSpec((1, V7X_LANES), lambda i, s: (0, 0))
    cache_in = pl.BlockSpec((None, bt, l, nk), lambda i, s: (layer, i, 0, 0))
    cache = pl.BlockSpec((bt, l, nk), lambda i, s: (i, 0, 0))
    vmem = 8 * bt * l * nk * 4 + (16 << 20)
    return pl.pallas_call(
        _swa_sample_kernel,
        grid_spec=pltpu.PrefetchScalarGridSpec(
            num_scalar_prefetch=1,
            grid=(b // bt,),
            in_specs=[pl.BlockSpec((bt * t, w), lambda i, s: (i, 0)), cache_in, cache_in, tab, tab, gain, gain],
            out_specs=[pl.BlockSpec((bt * t, nq), lambda i, s: (i, 0)), cache, cache],
            scratch_shapes=[pltpu.VMEM((pad, nk), F32), pltpu.VMEM((pad, nk), F32)],
        ),
        out_shape=[
            jax.ShapeDtypeStruct((m, nq), F32),
            jax.ShapeDtypeStruct((b, l, nk), F32),
            jax.ShapeDtypeStruct((b, l, nk), F32),
        ],
        compiler_params=_params(("parallel",), vmem),
        name="swa_sample",
    )(sinks, proj, cache_k, cache_v, cos, sin, gq, gk)


def _ret_tables(pos):
    inv = 1.0 / (ROPE_THETA ** jnp.linspace(0.0, 1.0, RET_DK // 2, dtype=F32))
    ang = pos[:, None] * inv[None, :]
    return jnp.cos(ang), jnp.sin(ang)


def _swa_tables(pos):
    inv = 1.0 / (ROPE_THETA ** (jnp.arange(0, SWA_HEAD_DIM, 2, dtype=F32) / SWA_HEAD_DIM))
    ang = pos[:, None] * inv[None, :]
    cos, sin = jnp.cos(ang), jnp.sin(ang)
    reps = V7X_LANES // SWA_HEAD_DIM
    return jnp.tile(jnp.concatenate([cos, cos], axis=1), (1, reps)), jnp.tile(jnp.concatenate([-sin, sin], axis=1), (1, reps))


def _q_head_permutation():
    idx = np.arange(SWA_Q_HEADS * SWA_HEAD_DIM).reshape(SWA_KV_HEADS, SWA_GROUP, SWA_HEAD_DIM)
    return idx.transpose(1, 0, 2).reshape(-1)


def kernel(x_prompt, x_sample, state_ret, cache_swa_k, cache_swa_v, norm_mix, norm_ffn, w_ret_in, w_ret_out,
           w_swa_in, w_swa_out, swa_q_norm, swa_k_norm, swa_sinks, w_ffn_in, w_ffn_out):
    b, t, d = x_prompt.shape
    db, dt, _ = x_sample.shape
    depth = norm_mix.shape[0]
    mp, ms = b * t, db * dt
    xp = x_prompt.reshape(mp, d)
    xs = x_sample.reshape(ms, d)
    pos_p = jnp.arange(t, dtype=F32)
    pos_s = PAST_LEN + jnp.arange(dt, dtype=F32)
    perm = _q_head_permutation()
    nq = SWA_Q_HEADS * SWA_HEAD_DIM
    nk = SWA_KV_HEADS * SWA_HEAD_DIM
    l = cache_swa_k.shape[2]
    cache_k = cache_swa_k.reshape(-1, db, l, nk)
    cache_v = cache_swa_v.reshape(-1, db, l, nk)
    tm_proj = min(1024, mp)
    tm_post = min(512, mp)

    ret_p, ret_s, kp, vp, ks, vs = [], [], [], [], [], []
    for layer in range(depth):
        if layer % 2 == 0:
            r = layer // 2
            w_in = w_ret_in[r].astype(BF16)
            w_out = w_ret_out[r].astype(BF16)
            cos_p, sin_p = _ret_tables(pos_p)
            cos_s, sin_s = _ret_tables(pos_s)
            proj_p = _norm_proj(xp, norm_mix[layer], w_in, tm=tm_proj, tn=2048)
            a_p, s_p = _ret_prompt(proj_p.reshape(b, t, -1), cos_p, sin_p, tt=min(512, t))
            proj_s = _norm_proj(xs, norm_mix[layer], w_in, tm=ms, tn=2048)
            a_s, s_s = _ret_sample(proj_s.reshape(db, dt, -1), state_ret, r, cos_s, sin_s, bt=2)
            ret_p.append(s_p)
            ret_s.append(s_s)
            a_p = a_p.reshape(mp, -1)
            a_s = a_s.reshape(ms, -1)
        else:
            a = layer // 2
            w_in = jnp.concatenate([w_swa_in[a][:, perm], w_swa_in[a][:, nq:]], axis=1).astype(BF16)
            w_out = w_swa_out[a][perm, :].astype(BF16)
            sinks = swa_sinks[a].reshape(SWA_KV_HEADS, SWA_GROUP).T.reshape(-1)
            reps = V7X_LANES // SWA_HEAD_DIM
            gq = jnp.tile(swa_q_norm[a], reps).reshape(1, V7X_LANES)
            gk = jnp.tile(swa_k_norm[a], reps).reshape(1, V7X_LANES)
            cos_p, sin_p = _swa_tables(pos_p)
            cos_s, sin_s = _swa_tables(pos_s)
            bt = 8
            cos_s, sin_s = jnp.tile(cos_s, (bt, 1)), jnp.tile(sin_s, (bt, 1))
            proj_p = _norm_proj(xp, norm_mix[layer], w_in, tm=tm_proj, tn=w_in.shape[1])
            a_p, k_p, v_p = _swa_prompt(proj_p.reshape(b, t, -1), sinks, cos_p, sin_p, gq, gk)
            proj_s = _norm_proj(xs, norm_mix[layer], w_in, tm=ms, tn=w_in.shape[1])
            a_s, k_s, v_s = _swa_sample(proj_s, cache_k, cache_v, a, sinks, cos_s, sin_s, gq, gk, bt=bt)
            kp.append(k_p.reshape(b, -1, SWA_KV_HEADS, SWA_HEAD_DIM))
            vp.append(v_p.reshape(b, -1, SWA_KV_HEADS, SWA_HEAD_DIM))
            ks.append(k_s.reshape(db, l, SWA_KV_HEADS, SWA_HEAD_DIM))
            vs.append(v_s.reshape(db, l, SWA_KV_HEADS, SWA_HEAD_DIM))
            a_p = a_p.reshape(mp, -1)
        w_fi = w_ffn_in[layer].astype(BF16)
        w_fo = w_ffn_out[layer].astype(BF16)
        xp = _post(xp, a_p, w_out, norm_ffn[layer], w_fi, w_fo, tm=tm_post, n_chunks=2)
        xs = _post(xs, a_s, w_out, norm_ffn[layer], w_fi, w_fo, tm=ms, n_chunks=2)

    def stack(parts):
        return parts[0][None] if len(parts) == 1 else jnp.stack(parts, 0)

    return (xp.reshape(b, t, d), xs.reshape(db, dt, d), stack(ret_p), stack(ret_s),
            stack(kp), stack(vp), stack(ks), stack(vs))
```
